```python
import jax, jax.numpy as jnp
from jax import lax
import numpy as np

D_MODEL = 1024
BATCH = 8
SEQ = 2048
DEPTH = 2
DEC_BATCH = 8
DEC_SEQ = 16
PAST_LEN = 2048

CHUNK = 64
N_EVEN = (DEPTH + 1) // 2
N_ODD = DEPTH // 2
D_A = D_MODEL // 2
SCONV_W = 3
H_B = 8
HD_B = D_MODEL // 2 // H_B
D_B = H_B * HD_B
Q_BLOCK = 128
D_C = D_MODEL // 2
C_GROUPS = 8
C_GW = D_C // C_GROUPS
C_CHUNK = 128
D_D = D_MODEL // 2
CCONV_W = 31
D_FF = 2816
FORGET_BIAS = 3.0
EPS = 1e-6
E_SPLITS = (D_A, 2 * D_A, 3 * D_A, 3 * D_A + D_B, 3 * D_A + 2 * D_B, 3 * D_A + 3 * D_B)
E_COLS = 3 * D_A + 3 * D_B + H_B
O_SPLITS = (D_C, 2 * D_C, 2 * D_C + D_D)
O_COLS = 2 * D_C + 2 * D_D

kernel_name = 'hybrid_streaming_encoder_step'


def rmsnorm(x, g):
    xf = x.astype(jnp.float32)
    y = xf * lax.rsqrt(jnp.mean(xf * xf, axis=-1, keepdims=True) + EPS)
    return y.astype(x.dtype) * g


def layernorm(x, g, b):
    xf = x.astype(jnp.float32)
    mu = jnp.mean(xf, axis=-1, keepdims=True)
    xc = xf - mu
    y = xc * lax.rsqrt(jnp.mean(xc * xc, axis=-1, keepdims=True) + EPS)
    return y.astype(x.dtype) * g + b


def swiglu(h, wg, wu, wd):
    return (jax.nn.silu(h @ wg) * (h @ wu)) @ wd


def causal_dwconv(x, ctx, w):
    xp = jnp.concatenate([ctx.astype(x.dtype), x], axis=1)
    y = lax.conv_general_dilated(xp, w[:, None, :].astype(x.dtype), (1,), 'VALID',
                                 dimension_numbers=('NWC', 'WIO', 'NWC'),
                                 feature_group_count=x.shape[-1])
    return y, xp[:, xp.shape[1] - (w.shape[0] - 1):]


def fox_block(q, c_q, q_pos, k, v, c_k):
    s = jnp.einsum('bqhd,bkhd->bhqk', q, k, preferred_element_type=jnp.float32) * (HD_B ** -0.5)
    s = s + jnp.transpose(c_q, (0, 2, 1))[:, :, :, None] - jnp.transpose(c_k, (0, 2, 1))[:, :, None, :]
    mask = jnp.arange(k.shape[1])[None, :] <= q_pos[:, None]
    s = jnp.where(mask[None, None], s, -jnp.inf)
    p = jax.nn.softmax(s, axis=-1).astype(v.dtype)
    return jnp.einsum('bhqk,bkhd->bqhd', p, v)


def even_mixer(h, w_in, b_f, conv_w, w_out, sconv_ctx, past):
    bsz, t, _ = h.shape
    z = h @ w_in
    xa, gb, gc, q, k, v, fl = jnp.split(z, E_SPLITS, axis=-1)
    ca, new_sconv = causal_dwconv(gc * xa, sconv_ctx, conv_w)
    ya = gb * ca
    q = q.reshape(bsz, t, H_B, HD_B)
    k = k.reshape(bsz, t, H_B, HD_B)
    v = v.reshape(bsz, t, H_B, HD_B)
    logf = jax.nn.log_sigmoid(fl.astype(jnp.float32) + b_f.astype(jnp.float32))
    if past is None:
        c = jnp.cumsum(logf, axis=1)
        nb = t // Q_BLOCK
        qb = jnp.transpose(q.reshape(bsz, nb, Q_BLOCK, H_B, HD_B), (1, 0, 2, 3, 4))
        cb = jnp.transpose(c.reshape(bsz, nb, Q_BLOCK, H_B), (1, 0, 2, 3))
        pos = jnp.arange(t).reshape(nb, Q_BLOCK)
        yb = lax.map(lambda a: fox_block(a[0], a[1], a[2], k, v, c), (qb, cb, pos))
        yb = jnp.transpose(yb, (1, 0, 2, 3, 4)).reshape(bsz, t, D_B)
    else:
        pk, pv, plf = past
        p_len = pk.shape[1]
        k_all = jnp.concatenate([pk.astype(k.dtype), k], axis=1)
        v_all = jnp.concatenate([pv.astype(v.dtype), v], axis=1)
        c_all = jnp.cumsum(jnp.concatenate([plf.astype(jnp.float32), logf], axis=1), axis=1)
        yb = fox_block(q, c_all[:, p_len:], p_len + jnp.arange(t), k_all, v_all, c_all)
        yb = yb.reshape(bsz, t, D_B)
    y = jnp.concatenate([ya, yb], axis=-1) @ w_out
    return y, k, v, logf, new_sconv


def spatial_gate(v, ws, bs):
    bsz, t, _ = v.shape
    L = min(t, C_CHUNK)
    n = t // L
    vr = v.reshape(bsz, n, L, C_GROUPS, C_GW)
    w = jnp.where(jnp.tril(jnp.ones((L, L), dtype=bool))[None], ws[:, :L, :L], 0).astype(v.dtype)
    s = jnp.einsum('gts,bnsgc->bntgc', w, vr) + jnp.transpose(bs[:, :L])[None, None, :, :, None].astype(v.dtype)
    return s.reshape(bsz, t, D_C)


def odd_mixer(h, w_in, ln_g, ln_b, ws, bs, dw, dw_b, cn_g, cn_b, w_out, cconv_ctx):
    z = h @ w_in
    u, vv, ga, gb = jnp.split(z, O_SPLITS, axis=-1)
    u = jax.nn.gelu(u)
    vv = layernorm(jax.nn.gelu(vv), ln_g, ln_b)
    yc = u * spatial_gate(vv, ws, bs)
    g = ga * jax.nn.sigmoid(gb)
    cd, new_cconv = causal_dwconv(g, cconv_ctx, dw)
    yd = jax.nn.silu(layernorm(cd + dw_b, cn_g, cn_b))
    y = jnp.concatenate([yc, yd], axis=-1) @ w_out
    return y, vv, new_cconv


def run_trunk(x, prm, sconv_ctx, cconv_ctx, past):
    ks, vs, lfs, scs, ccs, gvs = [], [], [], [], [], []
    for l in range(DEPTH):
        i = l // 2
        x = x + 0.5 * swiglu(rmsnorm(x, prm['ffn1_g'][l]), prm['ffn1_wg'][l], prm['ffn1_wu'][l], prm['ffn1_wd'][l])
        h = rmsnorm(x, prm['mix_g'][l])
        if l % 2 == 0:
            layer_past = None if past is None else (past[0][i], past[1][i], past[2][i])
            y, k, v, lf, sc = even_mixer(h, prm['e_w_in'][i], prm['e_b_f'][i], prm['e_conv_w'][i],
                                         prm['e_w_out'][i], sconv_ctx[i], layer_past)
            ks.append(k); vs.append(v); lfs.append(lf); scs.append(sc)
        else:
            y, gv, cc = odd_mixer(h, prm['o_w_in'][i], prm['o_ln_g'][i], prm['o_ln_b'][i], prm['o_ws'][i],
                                  prm['o_bs'][i], prm['o_dw'][i], prm['o_dw_b'][i], prm['o_cn_g'][i],
                                  prm['o_cn_b'][i], prm['o_w_out'][i], cconv_ctx[i])
            gvs.append(gv); ccs.append(cc)
        x = x + y
        x = x + 0.5 * swiglu(rmsnorm(x, prm['ffn2_g'][l]), prm['ffn2_wg'][l], prm['ffn2_wu'][l], prm['ffn2_wd'][l])
    return (rmsnorm(x, prm['final_g']), jnp.stack(ks), jnp.stack(vs), jnp.stack(lfs),
            jnp.stack(scs), jnp.stack(ccs), jnp.stack(gvs))


def setup_inputs(seed: int = 0) -> dict:
    key = jax.random.key(seed)
    ks = jax.random.split(key, 40)
    f32 = jnp.float32

    def nrm(k, shape, scale):
        return jax.random.normal(k, shape, f32) * scale

    D = D_MODEL
    return {
        'x_prompt': nrm(ks[0], (BATCH, SEQ, D), 1.0),
        'x_sample': nrm(ks[1], (DEC_BATCH, DEC_SEQ, D), 1.0),
        'cache_fox_k': nrm(ks[2], (N_EVEN, DEC_BATCH, PAST_LEN, H_B, HD_B), 1.0),
        'cache_fox_v': nrm(ks[3], (N_EVEN, DEC_BATCH, PAST_LEN, H_B, HD_B), 1.0),
        'cache_fox_logf': jax.nn.log_sigmoid(FORGET_BIAS + nrm(ks[4], (N_EVEN, DEC_BATCH, PAST_LEN, H_B), 1.0)),
        'state_sconv': nrm(ks[5], (N_EVEN, DEC_BATCH, SCONV_W - 1, D_A), 1.0),
        'state_cconv': nrm(ks[6], (N_ODD, DEC_BATCH, CCONV_W - 1, D_D), 0.5),
        'ffn1_g': 1.0 + nrm(ks[7], (DEPTH, D), 0.02),
        'ffn1_wg': nrm(ks[8], (DEPTH, D, D_FF), D ** -0.5),
        'ffn1_wu': nrm(ks[9], (DEPTH, D, D_FF), D ** -0.5),
        'ffn1_wd': nrm(ks[10], (DEPTH, D_FF, D), D_FF ** -0.5),
        'mix_g': 1.0 + nrm(ks[11], (DEPTH, D), 0.02),
        'ffn2_g': 1.0 + nrm(ks[12], (DEPTH, D), 0.02),
        'ffn2_wg': nrm(ks[13], (DEPTH, D, D_FF), D ** -0.5),
        'ffn2_wu': nrm(ks[14], (DEPTH, D, D_FF), D ** -0.5),
        'ffn2_wd': nrm(ks[15], (DEPTH, D_FF, D), D_FF ** -0.5),
        'e_w_in': nrm(ks[16], (N_EVEN, D, E_COLS), D ** -0.5),
        'e_b_f': FORGET_BIAS + nrm(ks[17], (N_EVEN, H_B), 0.1),
        'e_conv_w': nrm(ks[18], (N_EVEN, SCONV_W, D_A), SCONV_W ** -0.5),
        'e_w_out': nrm(ks[19], (N_EVEN, D_A + D_B, D), (D_A + D_B) ** -0.5),
        'o_w_in': nrm(ks[20], (N_ODD, D, O_COLS), D ** -0.5),
        'o_ln_g': 1.0 + nrm(ks[21], (N_ODD, D_C), 0.02),
        'o_ln_b': nrm(ks[22], (N_ODD, D_C), 0.02),
        'o_ws': nrm(ks[23], (N_ODD, C_GROUPS, C_CHUNK, C_CHUNK), 0.5 * C_CHUNK ** -0.5),
        'o_bs': 1.0 + nrm(ks[24], (N_ODD, C_GROUPS, C_CHUNK), 0.02),
        'o_dw': nrm(ks[25], (N_ODD, CCONV_W, D_D), CCONV_W ** -0.5),
        'o_dw_b': nrm(ks[26], (N_ODD, D_D), 0.02),
        'o_cn_g': 1.0 + nrm(ks[27], (N_ODD, D_D), 0.02),
        'o_cn_b': nrm(ks[28], (N_ODD, D_D), 0.02),
        'o_w_out': nrm(ks[29], (N_ODD, D_C + D_D, D), (D_C + D_D) ** -0.5),
        'final_g': 1.0 + nrm(ks[30], (D,), 0.02),
    }


def reference(x_prompt, x_sample, cache_fox_k, cache_fox_v, cache_fox_logf, state_sconv, state_cconv,
              ffn1_g, ffn1_wg, ffn1_wu, ffn1_wd, mix_g, ffn2_g, ffn2_wg, ffn2_wu, ffn2_wd,
              e_w_in, e_b_f, e_conv_w, e_w_out, o_w_in, o_ln_g, o_ln_b, o_ws, o_bs,
              o_dw, o_dw_b, o_cn_g, o_cn_b, o_w_out, final_g):
    prm = dict(ffn1_g=ffn1_g, ffn1_wg=ffn1_wg, ffn1_wu=ffn1_wu, ffn1_wd=ffn1_wd, mix_g=mix_g,
               ffn2_g=ffn2_g, ffn2_wg=ffn2_wg, ffn2_wu=ffn2_wu, ffn2_wd=ffn2_wd,
               e_w_in=e_w_in, e_b_f=e_b_f, e_conv_w=e_conv_w, e_w_out=e_w_out,
               o_w_in=o_w_in, o_ln_g=o_ln_g, o_ln_b=o_ln_b, o_ws=o_ws, o_bs=o_bs,
               o_dw=o_dw, o_dw_b=o_dw_b, o_cn_g=o_cn_g, o_cn_b=o_cn_b, o_w_out=o_w_out,
               final_g=final_g)
    bp = x_prompt.shape[0]
    zero_sconv = jnp.zeros((N_EVEN, bp, SCONV_W - 1, D_A), x_prompt.dtype)
    zero_cconv = jnp.zeros((N_ODD, bp, CCONV_W - 1, D_D), x_prompt.dtype)
    y_prompt, p_fox_k, p_fox_v, p_fox_logf, p_sconv, p_cconv, _p_gv = run_trunk(
        x_prompt, prm, zero_sconv, zero_cconv, None)
    y_sample, s_fox_k, s_fox_v, s_fox_logf, s_sconv, s_cconv, s_gmlp_v = run_trunk(
        x_sample, prm, state_sconv, state_cconv, (cache_fox_k, cache_fox_v, cache_fox_logf))
    return (y_prompt, y_sample, p_fox_k, p_fox_v, p_fox_logf, p_sconv, p_cconv,
            s_fox_k, s_fox_v, s_fox_logf, s_sconv, s_cconv, s_gmlp_v)
```

```python
import functools

import jax
import jax.numpy as jnp
from jax import lax
from jax.experimental import pallas as pl
from jax.experimental.pallas import tpu as pltpu

EPS = 1e-6
F32 = jnp.float32
BF16 = jnp.bfloat16

V7X_LANES = 128
V7X_SUBLANES = 8
V7X_MXU_DIM = 256
V7X_VMEM_BYTES = 64 * 1024 * 1024

C_CHUNK = 128
ROW_TILE = 512
Q_TILE = 256
K_TILE = 256
FF_CHUNK = 256


def _round_up(n, m):
    return (n + m - 1) // m * m


def _dot(a, b):
    return jnp.dot(a, b, preferred_element_type=F32)


def _dot_nt(a, b):
    return lax.dot_general(a, b, (((1,), (1,)), ((), ())), preferred_element_type=F32)


def _rms(x, g):
    return x * lax.rsqrt(jnp.mean(x * x, axis=-1, keepdims=True) + EPS) * g


def _layernorm(x, g, b):
    mu = jnp.mean(x, axis=-1, keepdims=True)
    xc = x - mu
    return xc * lax.rsqrt(jnp.mean(xc * xc, axis=-1, keepdims=True) + EPS) * g + b


def _log_sigmoid(x):
    return jnp.minimum(x, 0.0) - jnp.log1p(jnp.exp(-jnp.abs(x)))


def _ffn(x, g_ref, wg_ref, wu_ref, wd_ref, a_scr):
    h = _rms(x, g_ref[...]).astype(BF16)
    d_ff = wg_ref.shape[1]
    for c in range(d_ff // FF_CHUNK):
        sl = pl.ds(c * FF_CHUNK, FF_CHUNK)
        gate = _dot(h, wg_ref[:, sl])
        up = _dot(h, wu_ref[:, sl])
        a_scr[:, sl] = (gate * jax.nn.sigmoid(gate) * up).astype(BF16)
    return x + 0.5 * _dot(a_scr[...], wd_ref[...])


def _causal_dwconv(piece, step, tiles_per_batch, rows, x, ctx_ref, w_ref, new_ref, ext_scr):
    taps = w_ref.shape[0]
    hist = taps - 1
    off = _round_up(hist, 8)

    def load_ctx():
        ext_scr[off - hist:off, :] = ctx_ref[piece]

    if tiles_per_batch > 1:
        pl.when(step % tiles_per_batch == 0)(load_ctx)
    else:
        load_ctx()
    ext_scr[off:off + rows, :] = x
    y = w_ref[0:1, :] * ext_scr[off - hist:off - hist + rows, :]
    for t in range(1, taps):
        y = y + w_ref[t:t + 1, :] * ext_scr[off - hist + t:off - hist + t + rows, :]
    new_ctx = ext_scr[off + rows - hist:off + rows, :]
    new_ref[piece] = new_ctx
    if tiles_per_batch > 1:
        ext_scr[off - hist:off, :] = new_ctx
    return y


def _even_in_kernel(x_ref, ctx_ref, g1_ref, wg_ref, wu_ref, wd_ref, mg_ref, win_ref, wfl_ref, wflt_ref,
                    bf_ref, bft_ref, cw_ref,
                    x1_ref, ya_ref, q_ref, k_ref, v_ref, kb_ref, vb_ref, lf_ref, lft_ref, ns_ref,
                    a_scr, ext_scr, *, tiles_per_batch, batches_per_tile, q_scale):
    step = pl.program_id(0)
    rows_all = x_ref.shape[0]
    rows = rows_all // batches_per_tile
    da = cw_ref.shape[1]

    x1 = _ffn(x_ref[...], g1_ref, wg_ref, wu_ref, wd_ref, a_scr)
    x1_ref[...] = x1
    h = _rms(x1, mg_ref[...]).astype(BF16)

    def proj(i):
        return _dot(h, win_ref[:, i * da:(i + 1) * da])

    gated = proj(2) * proj(0)
    gb = proj(1)
    for j in range(batches_per_tile):
        rs = slice(j * rows, (j + 1) * rows)
        ca = _causal_dwconv(j, step, tiles_per_batch, rows, gated[rs], ctx_ref, cw_ref, ns_ref, ext_scr)
        ya_ref[rs, :] = (gb[rs] * ca).astype(BF16)
    q_ref[...] = (proj(3) * q_scale).astype(BF16)
    k = proj(4)
    k_ref[...] = k
    kb_ref[...] = k.astype(BF16)
    v = proj(5)
    v_ref[...] = v
    vb_ref[...] = v.astype(BF16)
    nh = lf_ref.shape[1]
    lf_ref[...] = _log_sigmoid(_dot(h, wfl_ref[...])[:, 0:nh] + bf_ref[...])
    lft_ref[...] = _log_sigmoid(_dot_nt(wflt_ref[...], h)[0:nh, :] + bft_ref[...])


def _const_spec(shape):
    nd = len(shape)
    return pl.BlockSpec(shape, lambda i: (0,) * nd, pipeline_mode=pl.Buffered(1))


def _vmem_limit(nbytes):
    return int(min(nbytes, V7X_VMEM_BYTES - 6 * 1024 * 1024))


def _ffn_specs(d, d_ff):
    return [_const_spec((1, d)), _const_spec((d, d_ff)), _const_spec((d, d_ff)), _const_spec((d_ff, d))]


def _tiling(n_batch, t):
    if t >= ROW_TILE:
        assert t % ROW_TILE == 0
        return ROW_TILE, t // ROW_TILE, 1
    return n_batch * t, 1, n_batch


def _even_in(x, ctx, ffn, mix_g, w_in, w_fl, w_flt, b_f, conv_w, n_batch, t, q_scale):
    m, d = x.shape
    d_ff = ffn[1].shape[1]
    da = conv_w.shape[1]
    nh = b_f.shape[-1]
    bm, tpb, bpt = _tiling(n_batch, t)
    hist = conv_w.shape[0] - 1
    row = lambda c: pl.BlockSpec((bm, c), lambda i: (i, 0))
    state_spec = pl.BlockSpec((bpt, hist, da), lambda i: (i // tpb, 0, 0))
    in_specs = ([row(d), state_spec] + _ffn_specs(d, d_ff)
                + [_const_spec((1, d)), _const_spec(w_in.shape), _const_spec(w_fl.shape), _const_spec(w_flt.shape),
                   _const_spec((1, nh)), _const_spec((nh, 1)), _const_spec(conv_w.shape)])
    out_shape = [jax.ShapeDtypeStruct((m, d), F32),
                 jax.ShapeDtypeStruct((m, da), BF16),
                 jax.ShapeDtypeStruct((m, da), BF16),
                 jax.ShapeDtypeStruct((m, da), F32),
                 jax.ShapeDtypeStruct((m, da), F32),
                 jax.ShapeDtypeStruct((m, da), BF16),
                 jax.ShapeDtypeStruct((m, da), BF16),
                 jax.ShapeDtypeStruct((m, nh), F32),
                 jax.ShapeDtypeStruct((nh, m), F32),
                 jax.ShapeDtypeStruct((n_batch, hist, da), F32)]
    out_specs = [row(d), row(da), row(da), row(da), row(da), row(da), row(da), row(nh),
                 pl.BlockSpec((nh, bm), lambda i: (0, i)), state_spec]
    kern = functools.partial(_even_in_kernel, tiles_per_batch=tpb, batches_per_tile=bpt, q_scale=q_scale)
    return pl.pallas_call(
        kern, grid=(m // bm,), in_specs=in_specs, out_specs=out_specs, out_shape=out_shape,
        scratch_shapes=[pltpu.VMEM((bm, d_ff), BF16),
                        pltpu.VMEM((_round_up(hist, 8) + bm // bpt, da), F32)],
        compiler_params=pltpu.CompilerParams(dimension_semantics=("arbitrary",),
                                             vmem_limit_bytes=_vmem_limit(58 * 1024 * 1024)),
        name="even_in",
    )(x, ctx, ffn[0], ffn[1], ffn[2], ffn[3], mix_g, w_in, w_fl, w_flt, b_f, b_f.T, conv_w)


def _cumsum_lanes(x):
    r, n = x.shape
    if n < V7X_MXU_DIM:
        lane = lax.broadcasted_iota(jnp.int32, (r, n), 1)
        out = jnp.zeros((r, n), F32)
        for s in range(n):
            out = out + jnp.where(lane >= s, x[:, s:s + 1], 0.0)
        return out
    blk = V7X_MXU_DIM
    assert n % blk == 0
    tri = jnp.where(lax.broadcasted_iota(jnp.int32, (blk, blk), 0)
                    <= lax.broadcasted_iota(jnp.int32, (blk, blk), 1), 1.0, 0.0).astype(BF16)
    carry = jnp.zeros((r, 1), F32)
    outs = []
    for b in range(n // blk):
        xb = x[:, b * blk:(b + 1) * blk]
        hi = xb.astype(BF16).astype(F32)
        r1 = xb - hi
        mid = r1.astype(BF16).astype(F32)
        lo = r1 - mid
        parts = _dot(jnp.concatenate([hi, mid, lo, jnp.zeros_like(hi)], axis=0).astype(BF16), tri)
        cb = parts[0:r] + parts[r:2 * r] + parts[2 * r:3 * r] + carry
        outs.append(cb)
        carry = cb[:, blk - 1:blk]
    return outs[0] if len(outs) == 1 else jnp.concatenate(outs, axis=1)


def _attn_prompt_kernel(q_ref, k_ref, v_ref, lft_ref, o_ref, negc_scr, *, hd):
    qi = pl.program_id(1)
    bq, dq = q_ref.shape
    nh = dq // hd
    pair = 2 * hd
    assert pair == V7X_LANES

    @pl.when(qi == 0)
    def _():
        negc_scr[...] = -_cumsum_lanes(lft_ref[...])

    lane = lax.broadcasted_iota(jnp.int32, (bq, pair), 1)
    qm = []
    for h in range(nh):
        qp = q_ref[:, (h // 2) * pair:(h // 2 + 1) * pair]
        keep = (lane < hd) if h % 2 == 0 else (lane >= hd)
        qm.append(jnp.where(keep, qp, jnp.zeros_like(qp)))
    causal = (lax.broadcasted_iota(jnp.int32, (bq, K_TILE), 1)
              <= lax.broadcasted_iota(jnp.int32, (bq, K_TILE), 0))

    def block(kb, carry, masked):
        ks = pl.multiple_of(kb * K_TILE, K_TILE)
        out = []
        for h in range(nh):
            ls = slice((h // 2) * pair, (h // 2 + 1) * pair)
            m, l, acc = carry[h]
            s = _dot_nt(qm[h], k_ref[pl.ds(ks, K_TILE), ls]) + negc_scr[h:h + 1, pl.ds(ks, K_TILE)]
            if masked:
                s = jnp.where(causal, s, -jnp.inf)
            m_new = jnp.maximum(m, jnp.max(s, axis=1, keepdims=True))
            alpha = jnp.exp(m - m_new)
            p = jnp.exp(s - m_new)
            l = alpha * l + jnp.sum(p, axis=1, keepdims=True)
            acc = alpha * acc + _dot(p.astype(BF16), v_ref[pl.ds(ks, K_TILE), ls])
            out.append((m_new, l, acc))
        return out

    init = [(jnp.full((bq, 1), -jnp.inf, F32), jnp.zeros((bq, 1), F32), jnp.zeros((bq, pair), F32))
            for _ in range(nh)]
    carry = lax.fori_loop(0, qi, lambda kb, c: block(kb, c, False), init)
    carry = block(qi, carry, True)
    for hp in range(nh // 2):
        o0 = carry[2 * hp][2] / carry[2 * hp][1]
        o1 = carry[2 * hp + 1][2] / carry[2 * hp + 1][1]
        o_ref[:, hp * pair:(hp + 1) * pair] = jnp.where(lane < hd, o0, o1).astype(BF16)


def _attn_prompt(q, kb, vb, lft, n_batch, t, hd):
    m, dq = q.shape
    nh = lft.shape[0]
    assert Q_TILE == K_TILE and t % Q_TILE == 0
    nq = t // Q_TILE
    kv_spec = pl.BlockSpec((t, dq), lambda b, i: (b, 0))
    return pl.pallas_call(
        functools.partial(_attn_prompt_kernel, hd=hd),
        grid=(n_batch, nq),
        in_specs=[pl.BlockSpec((Q_TILE, dq), lambda b, i: (b * nq + i, 0)), kv_spec, kv_spec,
                  pl.BlockSpec((nh, t), lambda b, i: (0, b))],
        out_specs=pl.BlockSpec((Q_TILE, dq), lambda b, i: (b * nq + i, 0)),
        out_shape=jax.ShapeDtypeStruct((m, dq), BF16),
        scratch_shapes=[pltpu.VMEM((nh, t), F32)],
        compiler_params=pltpu.CompilerParams(dimension_semantics=("arbitrary", "arbitrary"),
                                             vmem_limit_bytes=_vmem_limit(40 * 1024 * 1024)),
        name="attn_prompt",
    )(q, kb, vb, lft)


def _attn_sample_kernel(q_ref, kn_ref, vn_ref, kc_ref, vc_ref, plft_ref, lft_ref, o_ref, *, hd):
    t, dq = q_ref.shape
    nh = dq // hd
    rows = nh * t
    past = kc_ref.shape[0]

    q = q_ref[...]
    qs = jnp.concatenate([q] * nh, axis=0)
    row_head = lax.broadcasted_iota(jnp.int32, (rows, dq), 0) // t
    lane_head = lax.broadcasted_iota(jnp.int32, (rows, dq), 1) // hd
    own = row_head == lane_head
    qs = jnp.where(own, qs, jnp.zeros_like(qs))

    c_past = _cumsum_lanes(plft_ref[...])
    c_new = _cumsum_lanes(lft_ref[...]) + c_past[:, past - 1:past]

    def per_head_rows(c):
        return jnp.concatenate([jnp.broadcast_to(c[h:h + 1, :], (t, c.shape[1])) for h in range(nh)], axis=0)

    s_past = _dot_nt(qs, kc_ref[...].astype(BF16)) - per_head_rows(c_past)
    s_new = _dot_nt(qs, kn_ref[...]) - per_head_rows(c_new)
    visible = (lax.broadcasted_iota(jnp.int32, (rows, t), 1)
               <= lax.broadcasted_iota(jnp.int32, (rows, t), 0) % t)
    s_new = jnp.where(visible, s_new, -jnp.inf)
    mx = jnp.maximum(jnp.max(s_past, axis=1, keepdims=True), jnp.max(s_new, axis=1, keepdims=True))
    p_past = jnp.exp(s_past - mx)
    p_new = jnp.exp(s_new - mx)
    denom = jnp.sum(p_past, axis=1, keepdims=True) + jnp.sum(p_new, axis=1, keepdims=True)
    o = (_dot(p_past.astype(BF16), vc_ref[...].astype(BF16)) + _dot(p_new.astype(BF16), vn_ref[...])) / denom
    o = jnp.where(own, o, 0.0)
    out = o[0:t]
    for h in range(1, nh):
        out = out + o[h * t:(h + 1) * t]
    o_ref[...] = out.astype(BF16)


def _attn_sample(q, kb, vb, cache_k, cache_v, cache_lft, lft, n_batch, t, hd):
    m, dq = q.shape
    nh = lft.shape[1]
    past = cache_k.shape[1]
    new_spec = pl.BlockSpec((t, dq), lambda b: (b, 0))
    cache_spec = pl.BlockSpec((None, past, dq), lambda b: (b, 0, 0))
    return pl.pallas_call(
        functools.partial(_attn_sample_kernel, hd=hd),
        grid=(n_batch,),
        in_specs=[new_spec, new_spec, new_spec, cache_spec, cache_spec,
                  pl.BlockSpec((None, nh, past), lambda b: (b, 0, 0)),
                  pl.BlockSpec((None, nh, t), lambda b: (b, 0, 0))],
        out_specs=new_spec,
        out_shape=jax.ShapeDtypeStruct((m, dq), BF16),
        compiler_params=pltpu.CompilerParams(dimension_semantics=("arbitrary",),
                                             vmem_limit_bytes=_vmem_limit(48 * 1024 * 1024)),
        name="attn_sample",
    )(q, kb, vb, cache_k, cache_v, cache_lft, lft)


def _mid_kernel(x_ref, ya_ref, yb_ref, wo_ref, g_ref, wg_ref, wu_ref, wd_ref, fg_ref, o_ref, a_scr, *, final):
    da = ya_ref.shape[1]
    x = x_ref[...] + _dot(ya_ref[...], wo_ref[0:da, :]) + _dot(yb_ref[...], wo_ref[da:, :])
    x = _ffn(x, g_ref, wg_ref, wu_ref, wd_ref, a_scr)
    if final:
        x = _rms(x, fg_ref[...])
    o_ref[...] = x


def _mid(x, ya, yb, w_out, ffn, final_g, n_batch, t, final):
    m, d = x.shape
    d_ff = ffn[1].shape[1]
    bm, _, _ = _tiling(n_batch, t)
    row = lambda c: pl.BlockSpec((bm, c), lambda i: (i, 0))
    return pl.pallas_call(
        functools.partial(_mid_kernel, final=final),
        grid=(m // bm,),
        in_specs=[row(d), row(ya.shape[1]), row(yb.shape[1]), _const_spec(w_out.shape)]
                 + _ffn_specs(d, d_ff) + [_const_spec((1, d))],
        out_specs=row(d),
        out_shape=jax.ShapeDtypeStruct((m, d), F32),
        scratch_shapes=[pltpu.VMEM((bm, d_ff), BF16)],
        compiler_params=pltpu.CompilerParams(dimension_semantics=("arbitrary",),
                                             vmem_limit_bytes=_vmem_limit(48 * 1024 * 1024)),
        name="mid_final" if final else "mid",
    )(x, ya, yb, w_out, ffn[0], ffn[1], ffn[2], ffn[3], final_g)


def _odd_in_kernel(x_ref, ctx_ref, g1_ref, wg_ref, wu_ref, wd_ref, mg_ref, win_ref, lng_ref, lnb_ref,
                   ws_ref, bsf_ref, dw_ref, dwb_ref, cng_ref, cnb_ref,
                   x1_ref, yc_ref, yd_ref, gv_ref, nc_ref,
                   a_scr, ext_scr, wt_scr, *, tiles_per_batch, batches_per_tile, chunk):
    step = pl.program_id(0)
    rows_all = x_ref.shape[0]
    rows = rows_all // batches_per_tile
    dc = yc_ref.shape[1]
    n_groups = ws_ref.shape[0]
    gw = dc // n_groups
    pair = 2 * gw
    assert pair == V7X_LANES

    @pl.when(step == 0)
    def _():
        lower = (lax.broadcasted_iota(jnp.int32, (chunk, chunk), 1)
                 <= lax.broadcasted_iota(jnp.int32, (chunk, chunk), 0))
        for g in range(n_groups):
            wt_scr[g] = jnp.where(lower, ws_ref[g, 0:chunk, 0:chunk], 0.0).astype(BF16)

    x1 = _ffn(x_ref[...], g1_ref, wg_ref, wu_ref, wd_ref, a_scr)
    x1_ref[...] = x1
    h = _rms(x1, mg_ref[...]).astype(BF16)

    def proj(i):
        return _dot(h, win_ref[:, i * dc:(i + 1) * dc])

    u = jax.nn.gelu(proj(0))
    vv = _layernorm(jax.nn.gelu(proj(1)), lng_ref[...], lnb_ref[...])
    gv_ref[...] = vv
    vvb = vv.astype(BF16)
    lane = lax.broadcasted_iota(jnp.int32, (chunk, pair), 1)
    for c in range(rows_all // chunk):
        rs = slice(c * chunk, (c + 1) * chunk)
        for gp in range(n_groups // 2):
            ls = slice(gp * pair, (gp + 1) * pair)
            vp = vvb[rs, ls]
            zero = jnp.zeros_like(vp)
            s = (_dot(wt_scr[2 * gp], jnp.where(lane < gw, vp, zero))
                 + _dot(wt_scr[2 * gp + 1], jnp.where(lane >= gw, vp, zero)))
            yc_ref[rs, ls] = (u[rs, ls] * (s + bsf_ref[0:chunk, ls])).astype(BF16)

    gate = proj(2) * jax.nn.sigmoid(proj(3))
    for j in range(batches_per_tile):
        rs = slice(j * rows, (j + 1) * rows)
        cd = _causal_dwconv(j, step, tiles_per_batch, rows, gate[rs], ctx_ref, dw_ref, nc_ref, ext_scr)
        yd_ref[rs, :] = jax.nn.silu(_layernorm(cd + dwb_ref[...], cng_ref[...], cnb_ref[...])).astype(BF16)


def _odd_in(x, ctx, ffn, mix_g, w_in, ln_g, ln_b, ws, bs_full, dw, dw_b, cn_g, cn_b, n_batch, t):
    m, d = x.shape
    d_ff = ffn[1].shape[1]
    dc = ln_g.shape[1]
    bm, tpb, bpt = _tiling(n_batch, t)
    chunk = min(t, C_CHUNK)
    hist = dw.shape[0] - 1
    n_groups = ws.shape[0]
    row = lambda c: pl.BlockSpec((bm, c), lambda i: (i, 0))
    state_spec = pl.BlockSpec((bpt, hist, dc), lambda i: (i // tpb, 0, 0))
    vec = _const_spec((1, dc))
    in_specs = ([row(d), state_spec] + _ffn_specs(d, d_ff)
                + [_const_spec((1, d)), _const_spec(w_in.shape), vec, vec, _const_spec(ws.shape),
                   _const_spec(bs_full.shape), _const_spec(dw.shape), vec, vec, vec])
    out_shape = [jax.ShapeDtypeStruct((m, d), F32),
                 jax.ShapeDtypeStruct((m, dc), BF16),
                 jax.ShapeDtypeStruct((m, dc), BF16),
                 jax.ShapeDtypeStruct((m, dc), F32),
                 jax.ShapeDtypeStruct((n_batch, hist, dc), F32)]
    out_specs = [row(d), row(dc), row(dc), row(dc), state_spec]
    kern = functools.partial(_odd_in_kernel, tiles_per_batch=tpb, batches_per_tile=bpt, chunk=chunk)
    return pl.pallas_call(
        kern, grid=(m // bm,), in_specs=in_specs, out_specs=out_specs, out_shape=out_shape,
        scratch_shapes=[pltpu.VMEM((bm, d_ff), BF16),
                        pltpu.VMEM((_round_up(hist, 8) + bm // bpt, dc), F32),
                        pltpu.VMEM((n_groups, chunk, chunk), BF16)],
        compiler_params=pltpu.CompilerParams(dimension_semantics=("arbitrary",),
                                             vmem_limit_bytes=_vmem_limit(58 * 1024 * 1024)),
        name="odd_in",
    )(x, ctx, ffn[0], ffn[1], ffn[2], ffn[3], mix_g, w_in, ln_g, ln_b, ws, bs_full, dw, dw_b, cn_g, cn_b)


def _trunk(x, prm, sconv_ctx, cconv_ctx, past):
    n_batch, t, d = x.shape
    m = n_batch * t
    nh, hd, da = prm["nh"], prm["hd"], prm["da"]
    xf = x.reshape(m, d)
    (x1, ya, q, k, v, kb, vb, lf, lft, new_sconv) = _even_in(
        xf, sconv_ctx, prm["ffn1"][0], prm["mix_g"][0], prm["e_w_in"], prm["e_w_fl"], prm["e_w_flt"], prm["e_b_f"],
        prm["e_conv_w"], n_batch, t, hd ** -0.5)
    if past is None:
        yb = _attn_prompt(q, kb, vb, lft, n_batch, t, hd)
    else:
        cache_k, cache_v, cache_lf = past
        p_len = cache_k.shape[1]
        lft_b = jnp.transpose(lft.reshape(nh, n_batch, t), (1, 0, 2))
        yb = _attn_sample(q, kb, vb, cache_k.reshape(n_batch, p_len, da), cache_v.reshape(n_batch, p_len, da),
                          jnp.swapaxes(cache_lf, 1, 2), lft_b, n_batch, t, hd)
    x2 = _mid(x1, ya, yb, prm["e_w_out"], prm["ffn2"][0], prm["final_g"], n_batch, t, False)
    (x3, yc, yd, gv, new_cconv) = _odd_in(
        x2, cconv_ctx, prm["ffn1"][1], prm["mix_g"][1], prm["o_w_in"], prm["o_ln_g"], prm["o_ln_b"],
        prm["o_ws"], prm["o_bs_full"], prm["o_dw"], prm["o_dw_b"], prm["o_cn_g"], prm["o_cn_b"], n_batch, t)
    y = _mid(x3, yc, yd, prm["o_w_out"], prm["ffn2"][1], prm["final_g"], n_batch, t, True)
    return (y.reshape(n_batch, t, d), k.reshape(1, n_batch, t, nh, hd), v.reshape(1, n_batch, t, nh, hd),
            lf.reshape(1, n_batch, t, nh), new_sconv[None], new_cconv[None], gv.reshape(1, n_batch, t, -1))


def kernel(x_prompt, x_sample, cache_fox_k, cache_fox_v, cache_fox_logf, state_sconv, state_cconv,
           ffn1_g, ffn1_wg, ffn1_wu, ffn1_wd, mix_g, ffn2_g, ffn2_wg, ffn2_wu, ffn2_wd,
           e_w_in, e_b_f, e_conv_w, e_w_out, o_w_in, o_ln_g, o_ln_b, o_ws, o_bs,
           o_dw, o_dw_b, o_cn_g, o_cn_b, o_w_out, final_g):
    depth, d = ffn1_g.shape
    assert depth == 2 and e_w_in.shape[0] == 1 and o_w_in.shape[0] == 1
    nh = e_b_f.shape[-1]
    da = e_conv_w.shape[-1]
    db = (e_w_in.shape[-1] - 3 * da - nh) // 3
    assert db == da and db % nh == 0
    n_groups, dc = o_ws.shape[1], o_ln_g.shape[-1]
    bf = lambda w: w.astype(BF16)
    vec = lambda a: a.reshape(1, -1)
    w_fl = e_w_in[0, :, 3 * da + 3 * db:]
    prm = dict(
        nh=nh, hd=db // nh, da=da,
        ffn1=[(vec(ffn1_g[l]), bf(ffn1_wg[l]), bf(ffn1_wu[l]), bf(ffn1_wd[l])) for l in range(depth)],
        ffn2=[(vec(ffn2_g[l]), bf(ffn2_wg[l]), bf(ffn2_wu[l]), bf(ffn2_wd[l])) for l in range(depth)],
        mix_g=[vec(mix_g[l]) for l in range(depth)],
        e_w_in=bf(e_w_in[0, :, :3 * da + 3 * db]),
        e_w_fl=jnp.pad(bf(w_fl), ((0, 0), (0, V7X_LANES - nh))),
        e_w_flt=jnp.pad(bf(w_fl).T, ((0, 2 * V7X_SUBLANES - nh), (0, 0))),
        e_b_f=vec(e_b_f[0]), e_conv_w=e_conv_w[0], e_w_out=bf(e_w_out[0]),
        o_w_in=bf(o_w_in[0]), o_ln_g=vec(o_ln_g[0]), o_ln_b=vec(o_ln_b[0]), o_ws=o_ws[0],
        o_bs_full=jnp.repeat(jnp.transpose(o_bs[0]), dc // n_groups, axis=1),
        o_dw=o_dw[0], o_dw_b=vec(o_dw_b[0]), o_cn_g=vec(o_cn_g[0]), o_cn_b=vec(o_cn_b[0]),
        o_w_out=bf(o_w_out[0]), final_g=vec(final_g),
    )
    bp = x_prompt.shape[0]
    zero_sconv = jnp.zeros((bp,) + state_sconv.shape[2:], x_prompt.dtype)
    zero_cconv = jnp.zeros((bp,) + state_cconv.shape[2:], x_prompt.dtype)
    (y_prompt, p_k, p_v, p_lf, p_sconv, p_cconv, _) = _trunk(x_prompt, prm, zero_sconv, zero_cconv, None)
    (y_sample, s_k, s_v, s_lf, s_sconv, s_cconv, s_gv) = _trunk(
        x_sample, prm, state_sconv[0], state_cconv[0], (cache_fox_k[0], cache_fox_v[0], cache_fox_logf[0]))
    return (y_prompt, y_sample, p_k, p_v, p_lf, p_sconv, p_cconv, s_k, s_v, s_lf, s_sconv, s_cconv, s_gv)
```
